```python
import math, functools
import jax
import jax.numpy as jnp
from jax import lax
import numpy as np

D_MODEL = 1024
BATCH = 2
SEQ = 8192
DEPTH = 4
DEC_BATCH = 128
DEC_SEQ = 8
PAST_LEN = 2048
PAGE_SIZE = 128

S5_WIDTH = 512
S5_GROUP = 16
S5_GROUPS = S5_WIDTH // S5_GROUP
S5_STATE = 64
ATT_HEADS = 8
HEAD_DIM = 64
ATT_WIDTH = ATT_HEADS * HEAD_DIM
MOBA_BLOCK = 256
MOBA_TOPK = 3
QUERY_CHUNK = 128
SSD_HEADS = 16
SSD_HEAD_DIM = 64
SSD_WIDTH = SSD_HEADS * SSD_HEAD_DIM
SSD_GROUPS = 2
SSD_STATE = 128
SSD_CONV = 4
SSD_CHUNK = 128
SSD_CONV_DIM = SSD_WIDTH + 2 * SSD_GROUPS * SSD_STATE
D_FF = 2816
N_EXPERTS = 8
TOP_K = 2
IN_SIZES = (S5_WIDTH, ATT_WIDTH, ATT_WIDTH, ATT_WIDTH, SSD_WIDTH, SSD_CONV_DIM, SSD_HEADS, D_MODEL, D_MODEL, D_MODEL)
N_IN = S5_WIDTH + 3 * ATT_WIDTH + SSD_WIDTH + SSD_CONV_DIM + SSD_HEADS + 3 * D_MODEL
EPS = 1e-6
NEG = -1e30

kernel_name = 'hybrid_s5_moba_ssd_decoder_step'


def _split_cols(z, sizes):
    out, start = [], 0
    for s in sizes:
        out.append(z[..., start:start + s])
        start += s
    return out


def _rmsnorm(x, w):
    xf = x.astype(jnp.float32)
    y = xf * lax.rsqrt(jnp.mean(xf * xf, axis=-1, keepdims=True) + EPS)
    return (y * w.astype(jnp.float32)).astype(x.dtype)


def _lin_combine(left, right):
    a1, b1 = left
    a2, b2 = right
    return a1 * a2, a2 * b1 + b2


def _s5(u, p, h0):
    f32 = jnp.float32
    bsz, t, _ = u.shape
    lam = lax.complex(p['s5_a_re'].astype(f32), p['s5_a_im'].astype(f32))
    dt = jnp.exp(p['s5_log_dt'].astype(f32))[:, None]
    a_bar = jnp.exp(lam * dt)
    coef = (a_bar - 1.0) / lam
    ug = u.astype(f32).reshape(bsz, t, S5_GROUPS, S5_GROUP)
    bu = lax.complex(jnp.einsum('btgc,gpc->btgp', ug, p['s5_b_re'].astype(f32)),
                     jnp.einsum('btgc,gpc->btgp', ug, p['s5_b_im'].astype(f32))) * coef
    bu = bu.at[:, 0].add(a_bar * h0)
    _, h = lax.associative_scan(_lin_combine, (jnp.broadcast_to(a_bar, bu.shape), bu), axis=1)
    y = (jnp.einsum('btgp,gcp->btgc', h.real, p['s5_c_re'].astype(f32))
         - jnp.einsum('btgp,gcp->btgc', h.imag, p['s5_c_im'].astype(f32)))
    y = y.reshape(bsz, t, S5_WIDTH) + p['s5_d'].astype(f32) * u.astype(f32)
    y = jax.nn.gelu(y)
    y = y * jax.nn.sigmoid(y @ p['s5_w_glu'].astype(f32) + p['s5_b_glu'].astype(f32))
    return y.astype(u.dtype), h[:, -1]


def _moba_core(q, kh, vh, k_means, q_pos, slopes):
    f32 = jnp.float32
    tq, nh, hd = q.shape
    nb = kh.shape[1]
    n_sel = min(MOBA_TOPK, nb)
    qf = q.astype(f32)
    own = q_pos // MOBA_BLOCK
    gate = jnp.einsum('thd,nhd->thn', qf, k_means)
    past = jnp.arange(nb)[None, None, :] < own[:, None, None]
    gate = jnp.where(past, gate, NEG)
    _, top_idx = lax.top_k(gate, n_sel)
    own_b = jnp.broadcast_to(own[:, None, None], (tq, nh, 1)).astype(top_idx.dtype)
    blk_idx = jnp.concatenate([top_idx, own_b], axis=-1)
    rank_ok = jnp.broadcast_to(jnp.arange(n_sel)[None, None, :] < own[:, None, None], (tq, nh, n_sel))
    blk_ok = jnp.concatenate([rank_ok, jnp.ones((tq, nh, 1), dtype=bool)], axis=-1)
    head = jnp.arange(nh)[None, :, None]
    kg = kh[head, blk_idx].astype(f32)
    vg = vh[head, blk_idx].astype(f32)
    logits = jnp.einsum('thd,thjsd->thjs', qf, kg) * (hd ** -0.5)
    k_pos = blk_idx[..., None] * MOBA_BLOCK + jnp.arange(MOBA_BLOCK)
    dist = q_pos[:, None, None, None] - k_pos
    logits = logits - slopes[None, :, None, None] * dist.astype(f32)
    mask = blk_ok[..., None] & (dist >= 0)
    logits = jnp.where(mask, logits, NEG)
    probs = jax.nn.softmax(logits.reshape(tq, nh, -1), axis=-1).reshape(logits.shape)
    out = jnp.einsum('thjs,thjsd->thd', probs, vg)
    return out.astype(q.dtype)


def _moba_prompt(q, k, v, slopes):
    bsz, t, nh, hd = q.shape
    nb = -(-t // MOBA_BLOCK)
    pad = nb * MOBA_BLOCK - t
    kb = jnp.pad(k, ((0, 0), (0, pad), (0, 0), (0, 0))).reshape(bsz, nb, MOBA_BLOCK, nh, hd)
    vb = jnp.pad(v, ((0, 0), (0, pad), (0, 0), (0, 0))).reshape(bsz, nb, MOBA_BLOCK, nh, hd)
    means = kb.astype(jnp.float32).mean(axis=2)
    kh = kb.transpose(0, 3, 1, 2, 4)
    vh = vb.transpose(0, 3, 1, 2, 4)
    nq = t // QUERY_CHUNK
    qc = q.reshape(bsz, nq, QUERY_CHUNK, nh, hd).transpose(1, 0, 2, 3, 4)
    pos = jnp.arange(t, dtype=jnp.int32).reshape(nq, QUERY_CHUNK)
    core_b = jax.vmap(_moba_core, in_axes=(0, 0, 0, 0, None, None))
    out = lax.map(lambda a: core_b(a[0], kh, vh, means, a[1], slopes), (qc, pos))
    return out.transpose(1, 0, 2, 3, 4).reshape(bsz, t, nh, hd)


def _moba_sample(q, k, v, cache_k, cache_v, layer, page_table, slopes):
    bsz, t, nh, hd = q.shape
    past = page_table.shape[1] * cache_k.shape[2]
    total = past + t
    nb = -(-total // MOBA_BLOCK)
    pad = nb * MOBA_BLOCK - total
    q_pos = past + jnp.arange(t, dtype=jnp.int32)

    def one(args):
        pt, qi, ki, vi = args
        kp = cache_k[layer, pt].reshape(past, nh, hd).astype(ki.dtype)
        vp = cache_v[layer, pt].reshape(past, nh, hd).astype(vi.dtype)
        kall = jnp.pad(jnp.concatenate([kp, ki], axis=0), ((0, pad), (0, 0), (0, 0))).reshape(nb, MOBA_BLOCK, nh, hd)
        vall = jnp.pad(jnp.concatenate([vp, vi], axis=0), ((0, pad), (0, 0), (0, 0))).reshape(nb, MOBA_BLOCK, nh, hd)
        means = kall.astype(jnp.float32).mean(axis=1)
        return _moba_core(qi, kall.transpose(2, 0, 1, 3), vall.transpose(2, 0, 1, 3), means, q_pos, slopes)

    return lax.map(one, (page_table, q, k, v))


def _causal_conv(xpad, w, b):
    y = lax.conv_general_dilated(xpad, w.astype(xpad.dtype)[:, None, :], window_strides=(1,), padding='VALID',
                                 dimension_numbers=('NWC', 'WIO', 'NWC'), feature_group_count=xpad.shape[-1])
    return y + b.astype(y.dtype)


def _segsum(a):
    n = a.shape[-1]
    cs = jnp.cumsum(a, axis=-1)
    diff = cs[..., :, None] - cs[..., None, :]
    tri = jnp.tril(jnp.ones((n, n), dtype=bool))
    return jnp.where(tri, diff, -jnp.inf)


def _ssd_scan(x, dt, a, bm, cm, h0):
    bsz, t, nh, hp = x.shape
    ng, ns = bm.shape[2], bm.shape[3]
    hpg = nh // ng
    cl = math.gcd(t, SSD_CHUNK)
    nc = t // cl
    xd = (x * dt[..., None]).reshape(bsz, nc, cl, ng, hpg, hp)
    da = (dt * a).reshape(bsz, nc, cl, ng, hpg).transpose(0, 3, 4, 1, 2)
    da_cs = jnp.cumsum(da, axis=-1)
    bc = bm.reshape(bsz, nc, cl, ng, ns)
    cc = cm.reshape(bsz, nc, cl, ng, ns)
    decay_in = jnp.exp(_segsum(da))
    cb = jnp.einsum('bclgn,bcsgn->bgcls', cc, bc)
    y_diag = jnp.einsum('bgjcls,bcsgjp->bclgjp', cb[:, :, None] * decay_in, xd)
    decay_end = jnp.exp(da_cs[..., -1:] - da_cs)
    states = jnp.einsum('bclgn,bgjcl,bclgjp->bcgjpn', bc, decay_end, xd)
    states = jnp.concatenate([h0.reshape(bsz, 1, ng, hpg, hp, ns), states], axis=1)
    chunk_a = jnp.pad(da_cs[..., -1], ((0, 0), (0, 0), (0, 0), (1, 0)))
    states = jnp.einsum('bgjzc,bcgjpn->bzgjpn', jnp.exp(_segsum(chunk_a)), states)
    y_off = jnp.einsum('bclgn,bcgjpn->bclgjp', cc, states[:, :-1]) * jnp.exp(da_cs).transpose(0, 3, 4, 1, 2)[..., None]
    y = (y_diag + y_off).reshape(bsz, t, nh, hp)
    return y, states[:, -1].reshape(bsz, nh, hp, ns)


def _mixer(h, p, s5_h0, conv_prefix, ssd_h0, attend):
    f32 = jnp.float32
    bsz, t, _ = h.shape
    u_s5, q, k, v, z, xbc, dt_raw, g_s5, g_att, g_ssd = _split_cols(h @ p['w_in'], IN_SIZES)
    y_s5, s5_last = _s5(u_s5, p, s5_h0)
    q = q.reshape(bsz, t, ATT_HEADS, HEAD_DIM)
    k = k.reshape(bsz, t, ATT_HEADS, HEAD_DIM)
    v = v.reshape(bsz, t, ATT_HEADS, HEAD_DIM)
    y_att = attend(q, k, v).reshape(bsz, t, ATT_WIDTH)
    xbc_full = jnp.concatenate([conv_prefix.astype(xbc.dtype), xbc], axis=1)
    conv_last = xbc_full[:, xbc_full.shape[1] - (SSD_CONV - 1):]
    xbc = jax.nn.silu(_causal_conv(xbc_full, p['ssd_conv_w'], p['ssd_conv_b']))
    xs, bm, cm = _split_cols(xbc, (SSD_WIDTH, SSD_GROUPS * SSD_STATE, SSD_GROUPS * SSD_STATE))
    dt = jax.nn.softplus(dt_raw.astype(f32) + p['ssd_dt_bias'].astype(f32))
    a = -jnp.exp(p['ssd_a_log'].astype(f32))
    xh = xs.astype(f32).reshape(bsz, t, SSD_HEADS, SSD_HEAD_DIM)
    y, ssd_last = _ssd_scan(xh, dt, a,
                            bm.astype(f32).reshape(bsz, t, SSD_GROUPS, SSD_STATE),
                            cm.astype(f32).reshape(bsz, t, SSD_GROUPS, SSD_STATE),
                            ssd_h0.astype(f32))
    y = (y + p['ssd_d'].astype(f32)[:, None] * xh).reshape(bsz, t, SSD_WIDTH) * jax.nn.silu(z.astype(f32))
    y_ssd = _rmsnorm(y, p['ssd_norm_w']).astype(h.dtype)
    merged = (jax.nn.sigmoid(g_s5) * (y_s5 @ p['proj_s5'])
              + jax.nn.sigmoid(g_att) * (y_att @ p['proj_att'])
              + jax.nn.sigmoid(g_ssd) * (y_ssd @ p['proj_ssd']))
    out = merged @ p['w_out']
    new_state = (k, v, s5_last.real.astype(h.dtype), s5_last.imag.astype(h.dtype), conv_last, ssd_last.astype(h.dtype))
    return out, new_state


def _swiglu(h, w1, w3, w2):
    return (jax.nn.silu(h @ w1) * (h @ w3)) @ w2


def _moe(h, router_w, router_b, w1, w3, w2):
    f32 = jnp.float32
    logits = (h @ router_w + router_b).astype(f32)
    top_val, top_idx = lax.top_k(logits, TOP_K)
    top_w = jax.nn.softmax(top_val, axis=-1)
    gates = jnp.sum(jax.nn.one_hot(top_idx, N_EXPERTS, dtype=f32) * top_w[..., None], axis=-2)
    out = jnp.zeros(h.shape, f32)
    for e in range(N_EXPERTS):
        out = out + gates[..., e:e + 1] * _swiglu(h, w1[e], w3[e], w2[e]).astype(f32)
    return out.astype(h.dtype)


def _layer(x, c, p, layer, s5_h0, conv_prefix, ssd_h0, attend):
    mod = jax.nn.silu(c) @ p['ada_w'] + p['ada_b']
    sh1, sc1, g1, sh2, sc2, g2 = jnp.split(mod[:, None, :], 6, axis=-1)
    h = _rmsnorm(x, p['norm1_w']) * (1 + sc1) + sh1
    mix, new_state = _mixer(h, p, s5_h0, conv_prefix, ssd_h0, attend)
    x = x + g1 * mix
    h = _rmsnorm(x, p['norm2_w']) * (1 + sc2) + sh2
    if layer % 2 == 0:
        ffn = _swiglu(h, p['ffn_w1'], p['ffn_w3'], p['ffn_w2'])
    else:
        ffn = _moe(h, p['router_w'], p['router_b'], p['moe_w1'], p['moe_w3'], p['moe_w2'])
    x = x + g2 * ffn
    return x, new_state


def setup_inputs(seed: int = 0) -> dict:
    key = jax.random.key(seed)
    ks = iter(jax.random.split(key, 64))
    f32 = jnp.float32
    D = D_MODEL
    n_pages = PAST_LEN // PAGE_SIZE
    used = DEC_BATCH * n_pages
    n_pool = used + max(1, used // 4)
    n_dense = (DEPTH + 1) // 2
    n_moe = DEPTH // 2

    def nrm(shape, scale):
        return scale * jax.random.normal(next(ks), shape, f32)

    x_prompt = nrm((BATCH, SEQ, D), 1.0)
    x_sample = nrm((DEC_BATCH, DEC_SEQ, D), 1.0)
    cache_k = nrm((DEPTH, n_pool, PAGE_SIZE, ATT_HEADS, HEAD_DIM), 1.0)
    cache_v = nrm((DEPTH, n_pool, PAGE_SIZE, ATT_HEADS, HEAD_DIM), 1.0)
    state_s5_re = nrm((DEPTH, DEC_BATCH, S5_GROUPS, S5_STATE), 0.05)
    state_s5_im = nrm((DEPTH, DEC_BATCH, S5_GROUPS, S5_STATE), 0.05)
    state_conv = nrm((DEPTH, DEC_BATCH, SSD_CONV - 1, SSD_CONV_DIM), 1.0)
    state_ssd = nrm((DEPTH, DEC_BATCH, SSD_HEADS, SSD_HEAD_DIM, SSD_STATE), 0.1)
    perm = jax.random.permutation(next(ks), n_pool)
    page_table = perm[:used].reshape(DEC_BATCH, n_pages).astype(jnp.int32)
    c_prompt = nrm((BATCH, D), 1.0)
    c_sample = nrm((DEC_BATCH, D), 1.0)

    ada_w = nrm((DEPTH, D, 6 * D), 0.5 * D ** -0.5)
    ada_b = nrm((DEPTH, 6 * D), 0.01)
    norm1_w = 1.0 + nrm((DEPTH, D), 0.01)
    norm2_w = 1.0 + nrm((DEPTH, D), 0.01)
    w_in = nrm((DEPTH, D, N_IN), D ** -0.5)
    s5_a_re = -0.5 + nrm((DEPTH, S5_GROUPS, S5_STATE), 0.01)
    s5_a_im = math.pi * jnp.arange(S5_STATE, dtype=f32) + nrm((DEPTH, S5_GROUPS, S5_STATE), 0.01)
    s5_log_dt = jax.random.uniform(next(ks), (DEPTH, S5_GROUPS), f32, math.log(1e-3), math.log(1e-1))
    s5_b_re = nrm((DEPTH, S5_GROUPS, S5_STATE, S5_GROUP), (2 * S5_GROUP) ** -0.5)
    s5_b_im = nrm((DEPTH, S5_GROUPS, S5_STATE, S5_GROUP), (2 * S5_GROUP) ** -0.5)
    s5_c_re = nrm((DEPTH, S5_GROUPS, S5_GROUP, S5_STATE), (2 * S5_STATE) ** -0.5)
    s5_c_im = nrm((DEPTH, S5_GROUPS, S5_GROUP, S5_STATE), (2 * S5_STATE) ** -0.5)
    s5_d = nrm((DEPTH, S5_WIDTH), 1.0)
    s5_w_glu = nrm((DEPTH, S5_WIDTH, S5_WIDTH), S5_WIDTH ** -0.5)
    s5_b_glu = nrm((DEPTH, S5_WIDTH), 0.01)
    ssd_conv_w = nrm((DEPTH, SSD_CONV, SSD_CONV_DIM), SSD_CONV ** -0.5)
    ssd_conv_b = nrm((DEPTH, SSD_CONV_DIM), 0.01)
    dt0 = jnp.exp(jax.random.uniform(next(ks), (DEPTH, SSD_HEADS), f32, math.log(1e-3), math.log(1e-1)))
    ssd_dt_bias = dt0 + jnp.log(-jnp.expm1(-dt0))
    ssd_a_log = jnp.log(jax.random.uniform(next(ks), (DEPTH, SSD_HEADS), f32, 1.0, 16.0))
    ssd_d = 1.0 + nrm((DEPTH, SSD_HEADS), 0.01)
    ssd_norm_w = 1.0 + nrm((DEPTH, SSD_WIDTH), 0.01)
    proj_s5 = nrm((DEPTH, S5_WIDTH, D), S5_WIDTH ** -0.5)
    proj_att = nrm((DEPTH, ATT_WIDTH, D), ATT_WIDTH ** -0.5)
    proj_ssd = nrm((DEPTH, SSD_WIDTH, D), SSD_WIDTH ** -0.5)
    w_out = nrm((DEPTH, D, D), D ** -0.5)
    ffn_w1 = nrm((n_dense, D, D_FF), D ** -0.5)
    ffn_w3 = nrm((n_dense, D, D_FF), D ** -0.5)
    ffn_w2 = nrm((n_dense, D_FF, D), D_FF ** -0.5)
    router_w = nrm((n_moe, D, N_EXPERTS), D ** -0.5)
    router_b = nrm((n_moe, N_EXPERTS), 0.01)
    moe_w1 = nrm((n_moe, N_EXPERTS, D, D_FF), D ** -0.5)
    moe_w3 = nrm((n_moe, N_EXPERTS, D, D_FF), D ** -0.5)
    moe_w2 = nrm((n_moe, N_EXPERTS, D_FF, D), D_FF ** -0.5)
    final_norm_w = 1.0 + nrm((D,), 0.01)
    return {'x_prompt': x_prompt, 'x_sample': x_sample, 'cache_k': cache_k, 'cache_v': cache_v,
            'state_s5_re': state_s5_re, 'state_s5_im': state_s5_im, 'state_conv': state_conv, 'state_ssd': state_ssd,
            'page_table': page_table, 'c_prompt': c_prompt, 'c_sample': c_sample,
            'ada_w': ada_w, 'ada_b': ada_b, 'norm1_w': norm1_w, 'norm2_w': norm2_w, 'w_in': w_in,
            's5_a_re': s5_a_re, 's5_a_im': s5_a_im, 's5_log_dt': s5_log_dt, 's5_b_re': s5_b_re, 's5_b_im': s5_b_im,
            's5_c_re': s5_c_re, 's5_c_im': s5_c_im, 's5_d': s5_d, 's5_w_glu': s5_w_glu, 's5_b_glu': s5_b_glu,
            'ssd_conv_w': ssd_conv_w, 'ssd_conv_b': ssd_conv_b, 'ssd_dt_bias': ssd_dt_bias, 'ssd_a_log': ssd_a_log,
            'ssd_d': ssd_d, 'ssd_norm_w': ssd_norm_w, 'proj_s5': proj_s5, 'proj_att': proj_att, 'proj_ssd': proj_ssd,
            'w_out': w_out, 'ffn_w1': ffn_w1, 'ffn_w3': ffn_w3, 'ffn_w2': ffn_w2, 'router_w': router_w,
            'router_b': router_b, 'moe_w1': moe_w1, 'moe_w3': moe_w3, 'moe_w2': moe_w2, 'final_norm_w': final_norm_w}


def reference(x_prompt, x_sample, cache_k, cache_v, state_s5_re, state_s5_im, state_conv, state_ssd,
              page_table, c_prompt, c_sample, ada_w, ada_b, norm1_w, norm2_w, w_in,
              s5_a_re, s5_a_im, s5_log_dt, s5_b_re, s5_b_im, s5_c_re, s5_c_im, s5_d, s5_w_glu, s5_b_glu,
              ssd_conv_w, ssd_conv_b, ssd_dt_bias, ssd_a_log, ssd_d, ssd_norm_w, proj_s5, proj_att, proj_ssd,
              w_out, ffn_w1, ffn_w3, ffn_w2, router_w, router_b, moe_w1, moe_w3, moe_w2, final_norm_w):
    f32 = jnp.float32
    slopes = 2.0 ** (-8.0 * jnp.arange(1, ATT_HEADS + 1, dtype=f32) / ATT_HEADS)
    bp = x_prompt.shape[0]
    xp, xs = x_prompt, x_sample
    st_p_all, st_s_all = [], []
    for l in range(DEPTH):
        p = {'ada_w': ada_w[l], 'ada_b': ada_b[l], 'norm1_w': norm1_w[l], 'norm2_w': norm2_w[l], 'w_in': w_in[l],
             's5_a_re': s5_a_re[l], 's5_a_im': s5_a_im[l], 's5_log_dt': s5_log_dt[l], 's5_b_re': s5_b_re[l],
             's5_b_im': s5_b_im[l], 's5_c_re': s5_c_re[l], 's5_c_im': s5_c_im[l], 's5_d': s5_d[l],
             's5_w_glu': s5_w_glu[l], 's5_b_glu': s5_b_glu[l], 'ssd_conv_w': ssd_conv_w[l], 'ssd_conv_b': ssd_conv_b[l],
             'ssd_dt_bias': ssd_dt_bias[l], 'ssd_a_log': ssd_a_log[l], 'ssd_d': ssd_d[l], 'ssd_norm_w': ssd_norm_w[l],
             'proj_s5': proj_s5[l], 'proj_att': proj_att[l], 'proj_ssd': proj_ssd[l], 'w_out': w_out[l]}
        if l % 2 == 0:
            p['ffn_w1'] = ffn_w1[l // 2]
            p['ffn_w3'] = ffn_w3[l // 2]
            p['ffn_w2'] = ffn_w2[l // 2]
        else:
            p['router_w'] = router_w[l // 2]
            p['router_b'] = router_b[l // 2]
            p['moe_w1'] = moe_w1[l // 2]
            p['moe_w3'] = moe_w3[l // 2]
            p['moe_w2'] = moe_w2[l // 2]
        attend_p = functools.partial(_moba_prompt, slopes=slopes)
        attend_s = functools.partial(_moba_sample, cache_k=cache_k, cache_v=cache_v, layer=l,
                                     page_table=page_table, slopes=slopes)
        xp, st_p = _layer(xp, c_prompt, p, l,
                          jnp.zeros((bp, S5_GROUPS, S5_STATE), jnp.complex64),
                          jnp.zeros((bp, SSD_CONV - 1, SSD_CONV_DIM), xp.dtype),
                          jnp.zeros((bp, SSD_HEADS, SSD_HEAD_DIM, SSD_STATE), f32),
                          attend_p)
        s5_h0 = lax.complex(state_s5_re[l].astype(f32), state_s5_im[l].astype(f32))
        xs, st_s = _layer(xs, c_sample, p, l, s5_h0, state_conv[l], state_ssd[l], attend_s)
        st_p_all.append(st_p)
        st_s_all.append(st_s)
    y_prompt = _rmsnorm(xp, final_norm_w)
    y_sample = _rmsnorm(xs, final_norm_w)
    k_prompt = jnp.stack([s[0] for s in st_p_all])
    v_prompt = jnp.stack([s[1] for s in st_p_all])
    s5_re_prompt = jnp.stack([s[2] for s in st_p_all])
    s5_im_prompt = jnp.stack([s[3] for s in st_p_all])
    conv_prompt = jnp.stack([s[4] for s in st_p_all])
    ssd_prompt = jnp.stack([s[5] for s in st_p_all])
    k_sample = jnp.stack([s[0] for s in st_s_all])
    v_sample = jnp.stack([s[1] for s in st_s_all])
    s5_re_sample = jnp.stack([s[2] for s in st_s_all])
    s5_im_sample = jnp.stack([s[3] for s in st_s_all])
    conv_sample = jnp.stack([s[4] for s in st_s_all])
    ssd_sample = jnp.stack([s[5] for s in st_s_all])
    return (y_prompt, y_sample, k_prompt, v_prompt, k_sample, v_sample,
            s5_re_prompt, s5_im_prompt, s5_re_sample, s5_im_sample,
            conv_prompt, conv_sample, ssd_prompt, ssd_sample)
```

```python
import functools
import math

import jax
import jax.numpy as jnp
from jax import lax
from jax.experimental import pallas as pl
from jax.experimental.pallas import tpu as pltpu

f32 = jnp.float32
bf16 = jnp.bfloat16
HIGHEST = lax.Precision.HIGHEST

D_MODEL = 1024
DEPTH = 4
PAGE_SIZE = 128
S5_WIDTH = 512
S5_GROUP = 16
S5_GROUPS = 32
S5_STATE = 64
S5_N = S5_GROUPS * S5_STATE
ATT_HEADS = 8
HEAD_DIM = 64
ATT_WIDTH = 512
MOBA_BLOCK = 256
MOBA_TOPK = 3
QUERY_CHUNK = 128
SSD_HEADS = 16
SSD_HEAD_DIM = 64
SSD_WIDTH = 1024
SSD_GROUPS = 2
SSD_STATE = 128
SSD_CONV = 4
SSD_CHUNK = 128
SSD_CONV_DIM = 1536
D_FF = 2816
N_EXPERTS = 8
EPS = 1e-6
NEG = -1e30
ATT_SCALE = HEAD_DIM ** -0.5

N_MAIN = 7680
COL_U, COL_Q, COL_K, COL_V = 0, 512, 1024, 1536
COL_Z, COL_GS5, COL_GATT, COL_GSSD, COL_XBC = 2048, 3072, 4096, 5120, 6144
DT_PAD = 128

VMEM_LIMIT_BYTES = 56 * 1024 * 1024
S5_SEG = 64
S5_COLS = 512
FFN_TF = 1408


def _params(*sem):
    return pltpu.CompilerParams(dimension_semantics=sem, vmem_limit_bytes=VMEM_LIMIT_BYTES)


def _dot(a, b):
    return jnp.dot(a, b, preferred_element_type=f32)


def _dot_nt(a, b, precision=None):
    return lax.dot_general(a, b, (((1,), (1,)), ((), ())), preferred_element_type=f32, precision=precision)


def _dot_tn(a, b):
    return lax.dot_general(a, b, (((0,), (0,)), ((), ())), preferred_element_type=f32)


def _silu(x):
    return x * jax.nn.sigmoid(x)


def _rms(x, w):
    return x * lax.rsqrt(jnp.mean(x * x, axis=-1, keepdims=True) + EPS) * w


def _ada_kernel(c_ref, w_ref, b_ref, o_ref):
    c = c_ref[...]
    o_ref[0] = _dot(_silu(c).astype(bf16), w_ref[0].astype(bf16)) + b_ref[0]


def _ada(c_all, ada_w, ada_b):
    rows = c_all.shape[0]
    tn = 1536
    return pl.pallas_call(
        _ada_kernel,
        grid=(DEPTH, 6 * D_MODEL // tn),
        in_specs=[pl.BlockSpec((rows, D_MODEL), lambda l, j: (0, 0)),
                  pl.BlockSpec((1, D_MODEL, tn), lambda l, j: (l, 0, j)),
                  pl.BlockSpec((1, 1, tn), lambda l, j: (l, 0, j))],
        out_specs=pl.BlockSpec((1, rows, tn), lambda l, j: (l, 0, j)),
        out_shape=jax.ShapeDtypeStruct((DEPTH, rows, 6 * D_MODEL), f32),
        compiler_params=_params("arbitrary", "arbitrary"),
        name="ada",
    )(c_all, ada_w, ada_b.reshape(DEPTH, 1, 6 * D_MODEL))


def _norm_matmul_kernel(x_ref, nw_ref, sc_ref, sh_ref, w_ref, o_ref, h_ref):
    bb, tt, _ = x_ref.shape

    @pl.when(pl.program_id(2) == 0)
    def _():
        h = _rms(x_ref[...], nw_ref[...]) * (1.0 + sc_ref[...]) + sh_ref[...]
        h_ref[...] = h.reshape(bb * tt, D_MODEL).astype(bf16)

    o_ref[...] = _dot(h_ref[...], w_ref[...]).reshape(o_ref.shape)


def _norm_matmul(x, nw, sc, sh, w, bb, tt, tn):
    b, t, _ = x.shape
    n = w.shape[1]
    return pl.pallas_call(
        _norm_matmul_kernel,
        grid=(b // bb, t // tt, n // tn),
        in_specs=[pl.BlockSpec((bb, tt, D_MODEL), lambda i, j, k: (i, j, 0)),
                  pl.BlockSpec((1, 1, D_MODEL), lambda i, j, k: (0, 0, 0)),
                  pl.BlockSpec((bb, 1, D_MODEL), lambda i, j, k: (i, 0, 0)),
                  pl.BlockSpec((bb, 1, D_MODEL), lambda i, j, k: (i, 0, 0)),
                  pl.BlockSpec((D_MODEL, tn), lambda i, j, k: (0, k))],
        out_specs=pl.BlockSpec((bb, tt, tn), lambda i, j, k: (i, j, k)),
        out_shape=jax.ShapeDtypeStruct((b, t, n), f32),
        scratch_shapes=[pltpu.VMEM((bb * tt, D_MODEL), bf16)],
        compiler_params=_params("arbitrary", "arbitrary", "arbitrary"),
        name="norm_matmul",
    )(x, nw.reshape(1, 1, D_MODEL), sc, sh, w)


def _s5_kernel(u_ref, h0r_ref, h0i_ref, bm_ref, cm_ref, ar_ref, ai_ref, pwr_ref, pwi_ref, d_ref, wg_ref, bg_ref,
               y_ref, sr_ref, si_ref, bu_ref, fin_ref, cin_ref, car_ref, *, rows, steps, chained):
    n = S5_N
    u = u_ref[0]
    bu_ref[...] = _dot(u.astype(bf16), bm_ref[...])

    if chained:
        @pl.when(pl.program_id(1) == 0)
        def _():
            cin_ref[...] = jnp.zeros(cin_ref.shape, f32)

    for cb in range(n // S5_COLS):
        re = slice(cb * S5_COLS, (cb + 1) * S5_COLS)
        im = slice(n + cb * S5_COLS, n + (cb + 1) * S5_COLS)
        ar = jnp.broadcast_to(ar_ref[:, re], (8, S5_COLS))
        ai = jnp.broadcast_to(ai_ref[:, re], (8, S5_COLS))
        for rb in range(rows // 8):
            rsl = slice(rb * 8, (rb + 1) * 8)
            if chained:
                hr0 = jnp.zeros((8, S5_COLS), f32)
                hi0 = jnp.zeros((8, S5_COLS), f32)
            else:
                hr0 = h0r_ref[rsl, re]
                hi0 = h0i_ref[rsl, re]

            def step(j, carry, rb=rb, re=re, im=im, ar=ar, ai=ai):
                hr, hi = carry
                r0 = pl.multiple_of(j * rows + rb * 8, 8)
                nhr = ar * hr - ai * hi + bu_ref[pl.ds(r0, 8), re]
                nhi = ar * hi + ai * hr + bu_ref[pl.ds(r0, 8), im]
                bu_ref[pl.ds(r0, 8), re] = nhr
                bu_ref[pl.ds(r0, 8), im] = nhi
                return nhr, nhi

            hr, hi = lax.fori_loop(0, steps, step, (hr0, hi0))
            fin_ref[rsl, re] = hr
            fin_ref[rsl, im] = hi

    if chained:
        alr = pwr_ref[steps - 1:steps, :]
        ali = pwi_ref[steps - 1:steps, :]
        cr = cin_ref[0:1, 0:n]
        ci = cin_ref[0:1, n:2 * n]
        for r in range(8):
            car_ref[r:r + 1, 0:n] = cr
            car_ref[r:r + 1, n:2 * n] = ci
            fr = fin_ref[r:r + 1, 0:n]
            fi = fin_ref[r:r + 1, n:2 * n]
            cr, ci = fr + alr * cr - ali * ci, fi + alr * ci + ali * cr
        cin_ref[0:1, 0:n] = cr
        cin_ref[0:1, n:2 * n] = ci
        sr_ref[0] = cr
        si_ref[0] = ci
        for cb in range(n // S5_COLS):
            re = slice(cb * S5_COLS, (cb + 1) * S5_COLS)
            im = slice(n + cb * S5_COLS, n + (cb + 1) * S5_COLS)
            c_re = car_ref[:, re]
            c_im = car_ref[:, im]

            def fix(j, carry, re=re, im=im, c_re=c_re, c_im=c_im):
                r0 = pl.multiple_of(j * 8, 8)
                pr = pwr_ref[pl.ds(j, 1), re]
                pi = pwi_ref[pl.ds(j, 1), re]
                bu_ref[pl.ds(r0, 8), re] = bu_ref[pl.ds(r0, 8), re] + (pr * c_re - pi * c_im)
                bu_ref[pl.ds(r0, 8), im] = bu_ref[pl.ds(r0, 8), im] + (pr * c_im + pi * c_re)
                return carry

            lax.fori_loop(0, steps, fix, 0)
    else:
        sr_ref[...] = fin_ref[:, 0:n]
        si_ref[...] = fin_ref[:, n:2 * n]

    y = _dot(bu_ref[...].astype(bf16), cm_ref[...]) + d_ref[...] * u
    y = jax.nn.gelu(y)
    y = y * jax.nn.sigmoid(_dot(y.astype(bf16), wg_ref[...]) + bg_ref[...])
    y_ref[0] = y.astype(y_ref.dtype)


def _s5_tables(p):
    lam = lax.complex(p['s5_a_re'], p['s5_a_im'])
    dt = jnp.exp(p['s5_log_dt'])[:, None]
    a_bar = jnp.exp(lam * dt)
    coef = (a_bar - 1.0) / lam
    b = lax.complex(p['s5_b_re'], p['s5_b_im']) * coef[:, :, None]
    eye = jnp.eye(S5_GROUPS, dtype=f32)
    bre = jnp.einsum('gpc,gh->gchp', b.real, eye).reshape(S5_WIDTH, S5_N)
    bim = jnp.einsum('gpc,gh->gchp', b.imag, eye).reshape(S5_WIDTH, S5_N)
    bmat = jnp.concatenate([bre, bim], axis=1).astype(bf16)
    cre = jnp.einsum('gcp,gh->gphc', p['s5_c_re'], eye).reshape(S5_N, S5_WIDTH)
    cim = jnp.einsum('gcp,gh->gphc', p['s5_c_im'], eye).reshape(S5_N, S5_WIDTH)
    cmat = jnp.concatenate([cre, -cim], axis=0).astype(bf16)
    steps = jnp.arange(1, S5_SEG + 1, dtype=f32)[:, None, None]
    pw = jnp.exp(lam[None] * dt[None] * steps).reshape(S5_SEG, S5_N)
    return dict(bmat=bmat, cmat=cmat, ar=a_bar.real.reshape(1, S5_N), ai=a_bar.imag.reshape(1, S5_N),
                pwr=pw.real, pwi=pw.imag, d=p['s5_d'].reshape(1, S5_WIDTH), wg=p['s5_w_glu'].astype(bf16),
                bg=p['s5_b_glu'].reshape(1, S5_WIDTH))


def _s5_call(u, h0r, h0i, tb, rows, steps, chained, n_outer, n_chunks):
    n = S5_N
    blk = steps * rows
    const = lambda i, c: (0, 0)
    if chained:
        st_shape = jax.ShapeDtypeStruct((n_outer, 1, n), f32)
        st_spec = pl.BlockSpec((1, 1, n), lambda i, c: (i, 0, 0))
        h0_spec = pl.BlockSpec((8, n), const)
    else:
        st_shape = jax.ShapeDtypeStruct((n_outer * rows, n), f32)
        st_spec = pl.BlockSpec((rows, n), lambda i, c: (i, 0))
        h0_spec = pl.BlockSpec((rows, n), lambda i, c: (i, 0))
    kern = functools.partial(_s5_kernel, rows=rows, steps=steps, chained=chained)
    return pl.pallas_call(
        kern,
        grid=(n_outer, n_chunks),
        in_specs=[pl.BlockSpec((1, blk, S5_WIDTH), lambda i, c: (i, c, 0)),
                  h0_spec, h0_spec,
                  pl.BlockSpec((S5_WIDTH, 2 * n), const),
                  pl.BlockSpec((2 * n, S5_WIDTH), const),
                  pl.BlockSpec((1, n), const), pl.BlockSpec((1, n), const),
                  pl.BlockSpec((S5_SEG, n), const), pl.BlockSpec((S5_SEG, n), const),
                  pl.BlockSpec((1, S5_WIDTH), const),
                  pl.BlockSpec((S5_WIDTH, S5_WIDTH), const),
                  pl.BlockSpec((1, S5_WIDTH), const)],
        out_specs=[pl.BlockSpec((1, blk, S5_WIDTH), lambda i, c: (i, c, 0)), st_spec, st_spec],
        out_shape=[jax.ShapeDtypeStruct(u.shape, bf16), st_shape, st_shape],
        scratch_shapes=[pltpu.VMEM((blk, 2 * n), f32), pltpu.VMEM((rows, 2 * n), f32),
                        pltpu.VMEM((8, 2 * n), f32), pltpu.VMEM((8, 2 * n), f32)],
        compiler_params=_params("arbitrary", "arbitrary"),
        name="s5",
    )(u, h0r, h0i, tb['bmat'], tb['cmat'], tb['ar'], tb['ai'], tb['pwr'], tb['pwi'], tb['d'], tb['wg'], tb['bg'])


def _s5_prompt(u, tb):
    b, t, _ = u.shape
    seg = S5_SEG
    chunk = 8 * seg
    nch = t // chunk
    up = u.reshape(b, nch, 8, seg, S5_WIDTH).transpose(0, 1, 3, 2, 4).reshape(b, t, S5_WIDTH)
    zero = jnp.zeros((8, S5_N), f32)
    y, sr, si = _s5_call(up, zero, zero, tb, 8, seg, True, b, nch)
    y = y.reshape(b, nch, seg, 8, S5_WIDTH).transpose(0, 1, 3, 2, 4).reshape(b, t, S5_WIDTH)
    return y, sr.reshape(b, S5_GROUPS, S5_STATE), si.reshape(b, S5_GROUPS, S5_STATE)


def _s5_sample(u, h0r, h0i, tb, rows=32):
    b, t, _ = u.shape
    nb = b // rows
    up = u.reshape(nb, rows, t, S5_WIDTH).transpose(0, 2, 1, 3).reshape(nb, t * rows, S5_WIDTH)
    y, sr, si = _s5_call(up, h0r.reshape(b, S5_N), h0i.reshape(b, S5_N), tb, rows, t, False, nb, 1)
    y = y.reshape(nb, t, rows, S5_WIDTH).transpose(0, 2, 1, 3).reshape(b, t, S5_WIDTH)
    return y, sr.reshape(b, S5_GROUPS, S5_STATE), si.reshape(b, S5_GROUPS, S5_STATE)


def _ssd_kernel(z_ref, xbc_ref, dt_ref, dtt_ref, pre_ref, h0_ref, cw_ref, cb_ref, dtb_ref, dtbt_ref, alog_ref,
                alogt_ref, drow_ref, nw_ref, y_ref, st_ref, xf_ref, yb_ref, *, chunk, carried):
    L = chunk
    if carried:
        @pl.when(pl.program_id(1) == 0)
        def _():
            xf_ref[0:8, :] = jnp.zeros((8, SSD_CONV_DIM), f32)
            st_ref[...] = jnp.zeros(st_ref.shape, f32)
    else:
        xf_ref[8 - (SSD_CONV - 1):8, :] = pre_ref[0, 0]
        st_ref[0] = h0_ref[0, 0]

    xf_ref[8:8 + L, :] = xbc_ref[0]
    acc = cb_ref[...] + cw_ref[0:1, :] * xf_ref[pl.ds(8 - (SSD_CONV - 1), L), :]
    for w in range(1, SSD_CONV):
        acc = acc + cw_ref[w:w + 1, :] * xf_ref[pl.ds(8 - (SSD_CONV - 1) + w, L), :]
    if carried:
        xf_ref[0:8, :] = xf_ref[L:L + 8, :]
    xc = _silu(acc)
    xs = xc[:, 0:SSD_WIDTH]
    gs = SSD_GROUPS * SSD_STATE

    dt = jax.nn.softplus(dt_ref[0] + dtb_ref[...])
    dtt = jax.nn.softplus(dtt_ref[0] + dtbt_ref[...])
    da = dt * (-jnp.exp(alog_ref[...]))
    dat = dtt * (-jnp.exp(alogt_ref[...]))
    li = lax.broadcasted_iota(jnp.int32, (L, L), 0)
    si = lax.broadcasted_iota(jnp.int32, (L, L), 1)
    tri = si <= li
    cs = jnp.dot(tri.astype(f32), da, preferred_element_type=f32, precision=HIGHEST)
    cst = jnp.dot(dat, (li <= si).astype(f32), preferred_element_type=f32, precision=HIGHEST)

    hpg = SSD_HEADS // SSD_GROUPS
    for g in range(SSD_GROUPS):
        bg = xc[:, SSD_WIDTH + g * SSD_STATE:SSD_WIDTH + (g + 1) * SSD_STATE].astype(bf16)
        cg = xc[:, SSD_WIDTH + gs + g * SSD_STATE:SSD_WIDTH + gs + (g + 1) * SSD_STATE].astype(bf16)
        cbm = _dot_nt(cg, bg)
        for j in range(hpg):
            h = g * hpg + j
            col = cs[:, h:h + 1]
            row = cst[h:h + 1, :]
            dec = jnp.where(tri, jnp.exp(col - row), 0.0)
            xh = xs[:, h * SSD_HEAD_DIM:(h + 1) * SSD_HEAD_DIM]
            xd = xh * dt[:, h:h + 1]
            s_in = st_ref[0, h]
            y_diag = _dot((cbm * dec).astype(bf16), xd.astype(bf16))
            y_off = _dot_nt(cg, s_in.astype(bf16)) * jnp.exp(col)
            last = cs[L - 1:L, h:h + 1]
            de = jnp.exp(last - col)
            st_ref[0, h] = jnp.exp(last) * s_in + _dot_tn((xd * de).astype(bf16), bg)
            yb_ref[:, h * SSD_HEAD_DIM:(h + 1) * SSD_HEAD_DIM] = y_diag + y_off

    y = (yb_ref[...] + drow_ref[...] * xs) * _silu(z_ref[0])
    y_ref[0] = _rms(y, nw_ref[...]).astype(y_ref.dtype)


def _ssd_call(yall, dt_raw, pre, h0, layer, p, chunk, carried):
    b, t, _ = yall.shape
    nch = t // chunk
    dtt = dt_raw.transpose(0, 2, 1)
    const = lambda i, c: (0, 0)
    pre_l = layer if not carried else 0
    kern = functools.partial(_ssd_kernel, chunk=chunk, carried=carried)
    pre_idx = (lambda i, c: (pre_l, i, 0, 0)) if not carried else (lambda i, c: (0, 0, 0, 0))
    h0_idx = (lambda i, c: (pre_l, i, 0, 0, 0)) if not carried else (lambda i, c: (0, 0, 0, 0, 0))
    return pl.pallas_call(
        kern,
        grid=(b, nch),
        in_specs=[pl.BlockSpec((1, chunk, SSD_WIDTH), lambda i, c: (i, c, COL_Z // SSD_WIDTH)),
                  pl.BlockSpec((1, chunk, SSD_CONV_DIM), lambda i, c: (i, c, COL_XBC // SSD_CONV_DIM)),
                  pl.BlockSpec((1, chunk, SSD_HEADS), lambda i, c: (i, c, 0)),
                  pl.BlockSpec((1, SSD_HEADS, chunk), lambda i, c: (i, 0, c)),
                  pl.BlockSpec((1, 1, SSD_CONV - 1, SSD_CONV_DIM), pre_idx),
                  pl.BlockSpec((1, 1, SSD_HEADS, SSD_HEAD_DIM, SSD_STATE), h0_idx),
                  pl.BlockSpec((SSD_CONV, SSD_CONV_DIM), const),
                  pl.BlockSpec((1, SSD_CONV_DIM), const),
                  pl.BlockSpec((1, SSD_HEADS), const),
                  pl.BlockSpec((SSD_HEADS, 1), const),
                  pl.BlockSpec((1, SSD_HEADS), const),
                  pl.BlockSpec((SSD_HEADS, 1), const),
                  pl.BlockSpec((1, SSD_WIDTH), const),
                  pl.BlockSpec((1, SSD_WIDTH), const)],
        out_specs=[pl.BlockSpec((1, chunk, SSD_WIDTH), lambda i, c: (i, c, 0)),
                   pl.BlockSpec((1, SSD_HEADS, SSD_HEAD_DIM, SSD_STATE), lambda i, c: (i, 0, 0, 0))],
        out_shape=[jax.ShapeDtypeStruct((b, t, SSD_WIDTH), bf16),
                   jax.ShapeDtypeStruct((b, SSD_HEADS, SSD_HEAD_DIM, SSD_STATE), f32)],
        scratch_shapes=[pltpu.VMEM((8 + chunk + (8 if chunk < 8 else 0), SSD_CONV_DIM), f32),
                        pltpu.VMEM((chunk, SSD_WIDTH), f32)],
        compiler_params=_params("arbitrary", "arbitrary"),
        name="ssd",
    )(yall, yall, dt_raw, dtt, pre, h0, p['ssd_conv_w'], p['ssd_conv_b'].reshape(1, -1),
      p['ssd_dt_bias'].reshape(1, -1), p['ssd_dt_bias'].reshape(-1, 1), p['ssd_a_log'].reshape(1, -1),
      p['ssd_a_log'].reshape(-1, 1), jnp.repeat(p['ssd_d'], SSD_HEAD_DIM).reshape(1, -1),
      p['ssd_norm_w'].reshape(1, -1))


def _kmeans_kernel(k_ref, o_ref):
    nb = o_ref.shape[1]
    o_ref[0] = k_ref[0].reshape(nb, MOBA_BLOCK, ATT_WIDTH).sum(axis=1) * (1.0 / MOBA_BLOCK)


def _kmeans(yall):
    b, t, _ = yall.shape
    nb = t // MOBA_BLOCK
    per = 8
    assert nb % per == 0
    return pl.pallas_call(
        _kmeans_kernel,
        grid=(b, nb // per),
        in_specs=[pl.BlockSpec((1, per * MOBA_BLOCK, ATT_WIDTH), lambda i, j: (i, j, COL_K // ATT_WIDTH))],
        out_specs=pl.BlockSpec((1, per, ATT_WIDTH), lambda i, j: (i, j, 0)),
        out_shape=jax.ShapeDtypeStruct((b, nb, ATT_WIDTH), f32),
        compiler_params=_params("arbitrary", "arbitrary"),
        name="kmeans",
    )(yall)


def _top3_rows(g, nidx, nb, valid_ranks):
    sel = jnp.zeros(g.shape, f32)
    for r in range(MOBA_TOPK):
        m = jnp.max(g, axis=-1, keepdims=True)
        idx = jnp.min(jnp.where(g == m, nidx, nb), axis=-1, keepdims=True)
        pick = nidx == idx
        keep = jnp.where(valid_ranks > r, 1.0, 0.0)
        sel = jnp.where(pick, keep, sel)
        g = jnp.where(pick, -jnp.inf, g)
    return sel


def _moba_prompt_kernel(q_ref, k_ref, v_ref, km_ref, o_ref, *, nb):
    qc = QUERY_CHUNK
    i = pl.program_id(1)
    own = i // (MOBA_BLOCK // qc)
    q = q_ref[0]
    km = km_ref[0]
    qpos = lax.broadcasted_iota(jnp.int32, (qc, 1), 0) + i * qc
    koff = lax.broadcasted_iota(jnp.int32, (1, MOBA_BLOCK), 1)
    nidx = lax.broadcasted_iota(jnp.int32, (qc, nb), 1)
    for h in range(ATT_HEADS):
        lo = h * HEAD_DIM
        qh = q[:, lo:lo + HEAD_DIM]
        gate = _dot_nt(qh, km[:, lo:lo + HEAD_DIM], precision=HIGHEST)
        sel = _top3_rows(jnp.where(nidx < own, gate, NEG), nidx, nb, own)
        qs = (qh * ATT_SCALE).astype(bf16)
        slope = 2.0 ** -(h + 1)

        def blk(n, carry, lo=lo, qs=qs, sel=sel, slope=slope):
            m_i, l_i, acc = carry
            start = pl.multiple_of(n * MOBA_BLOCK, MOBA_BLOCK)
            kb = k_ref[0, pl.ds(start, MOBA_BLOCK), lo:lo + HEAD_DIM]
            vb = v_ref[0, pl.ds(start, MOBA_BLOCK), lo:lo + HEAD_DIM]
            dist = qpos - (koff + n * MOBA_BLOCK)
            s = _dot_nt(qs, kb) - slope * dist.astype(f32)
            is_own = n == own
            picked = jnp.sum(jnp.where(nidx == n, sel, 0.0), axis=-1, keepdims=True) + jnp.where(is_own, 1.0, 0.0)
            causal = dist >= jnp.where(is_own, 0, -(1 << 30))
            s = jnp.where(causal & (picked > 0.5), s, NEG)
            m_new = jnp.maximum(m_i, jnp.max(s, axis=-1, keepdims=True))
            alpha = jnp.exp(m_i - m_new)
            pexp = jnp.exp(s - m_new)
            l_new = alpha * l_i + jnp.sum(pexp, axis=-1, keepdims=True)
            acc = alpha * acc + _dot(pexp.astype(bf16), vb)
            return m_new, l_new, acc

        init = (jnp.full((qc, 1), NEG, f32), jnp.zeros((qc, 1), f32), jnp.zeros((qc, HEAD_DIM), f32))
        _, l_f, acc = lax.fori_loop(0, own + 1, blk, init)
        o_ref[0, :, lo:lo + HEAD_DIM] = (acc / l_f).astype(o_ref.dtype)


def _moba_prompt(yall):
    b, t, _ = yall.shape
    nb = t // MOBA_BLOCK
    kb = yall[:, :, COL_K:COL_K + ATT_WIDTH].astype(bf16)
    vb = yall[:, :, COL_V:COL_V + ATT_WIDTH].astype(bf16)
    km = _kmeans(yall)
    return pl.pallas_call(
        functools.partial(_moba_prompt_kernel, nb=nb),
        grid=(b, t // QUERY_CHUNK),
        in_specs=[pl.BlockSpec((1, QUERY_CHUNK, ATT_WIDTH), lambda i, j: (i, j, COL_Q // ATT_WIDTH)),
                  pl.BlockSpec((1, t, ATT_WIDTH), lambda i, j: (i, 0, 0)),
                  pl.BlockSpec((1, t, ATT_WIDTH), lambda i, j: (i, 0, 0)),
                  pl.BlockSpec((1, nb, ATT_WIDTH), lambda i, j: (i, 0, 0))],
        out_specs=pl.BlockSpec((1, QUERY_CHUNK, ATT_WIDTH), lambda i, j: (i, j, 0)),
        out_shape=jax.ShapeDtypeStruct((b, t, ATT_WIDTH), bf16),
        compiler_params=_params("arbitrary", "arbitrary"),
        name="moba_prompt",
    )(yall, kb, vb, km)


def _moba_sample_kernel(pt_ref, qa_ref, kn_ref, vn_ref, kc_ref, vc_ref, o_ref, m_ref, l_ref, acc_ref, ks_ref, *,
                        n_pages, n_new):
    del pt_ref
    j = pl.program_id(1)
    nh, hd = ATT_HEADS, HEAD_DIM
    past = n_pages * PAGE_SIZE
    ncol = nh * n_new
    qa = qa_ref[0]
    qs = (qa * ATT_SCALE).astype(bf16)
    coli = lax.broadcasted_iota(jnp.int32, (1, ncol), 1)
    ch = coli // n_new
    ct = coli - ch * n_new
    slope = jnp.zeros((1, ncol), f32)
    for h in range(nh):
        slope = jnp.where(ch == h, 2.0 ** -(h + 1), slope)

    def partial_softmax(kk, vv, kpos, n_rows):
        rowi = lax.broadcasted_iota(jnp.int32, (n_rows, 1), 0)
        rh = rowi % nh
        dist = (past + ct) - kpos(rowi // nh)
        s = _dot_nt(kk.astype(bf16), qs) - slope * dist.astype(f32)
        s = jnp.where((rh == ch) & (dist >= 0), s, NEG)
        m = jnp.max(s, axis=0, keepdims=True)
        pexp = jnp.exp(s - m)
        return m, jnp.sum(pexp, axis=0, keepdims=True), _dot_tn(pexp.astype(bf16), vv.astype(bf16))

    kp = kc_ref[0, 0]
    m, l, acc = partial_softmax(kp, vc_ref[0, 0], lambda r: r + j * PAGE_SIZE, PAGE_SIZE * nh)
    m_ref[pl.ds(j, 1), :] = m
    l_ref[pl.ds(j, 1), :] = l
    acc_ref[j] = acc
    ks_ref[j] = kp.reshape(PAGE_SIZE, nh, hd).sum(axis=0)

    @pl.when(j == n_pages - 1)
    def _():
        ppb = MOBA_BLOCK // PAGE_SIZE
        nbp = n_pages // ppb
        means = ks_ref[...].reshape(nbp, ppb, nh, hd).sum(axis=1) * (1.0 / MOBA_BLOCK)
        gfull = _dot_nt(means.reshape(nbp * nh, hd), qa, precision=HIGHEST)
        rowh = lax.broadcasted_iota(jnp.int32, (nbp * nh, 1), 0) % nh
        gate = jnp.where(rowh == ch, gfull, 0.0).reshape(nbp, nh, ncol).sum(axis=1)
        nidx = lax.broadcasted_iota(jnp.int32, (nbp, ncol), 0)
        sel = jnp.zeros((nbp, ncol), f32)
        g = gate
        for _ in range(MOBA_TOPK):
            mx = jnp.max(g, axis=0, keepdims=True)
            idx = jnp.min(jnp.where(g == mx, nidx, nbp), axis=0, keepdims=True)
            pick = nidx == idx
            sel = jnp.where(pick, 1.0, sel)
            g = jnp.where(pick, -jnp.inf, g)
        pj = lax.broadcasted_iota(jnp.int32, (n_pages, nbp), 0) // ppb
        pn = lax.broadcasted_iota(jnp.int32, (n_pages, nbp), 1)
        selp = _dot((pj == pn).astype(f32), sel) > 0.5

        mo, lo, acco = partial_softmax(kn_ref[0], vn_ref[0], lambda r: r + past, n_new * nh)
        m_all = m_ref[...]
        mtot = jnp.maximum(mo, jnp.max(jnp.where(selp, m_all, -jnp.inf), axis=0, keepdims=True))
        w_all = jnp.where(selp, jnp.exp(m_all - mtot), 0.0)
        wo = jnp.exp(mo - mtot)
        lsum = wo * lo + jnp.sum(w_all * l_ref[...], axis=0, keepdims=True)
        pad = jnp.zeros((6, ncol), f32)
        wt = jnp.concatenate([w_all, wo, lsum, pad], axis=0).T
        num = wt[:, n_pages:n_pages + 1] * acco
        for jj in range(n_pages):
            num = num + wt[:, jj:jj + 1] * acc_ref[jj]
        o_ref[0] = num / wt[:, n_pages + 1:n_pages + 2]


def _moba_sample(yall, cache_k, cache_v, page_table, layer):
    b, tn, _ = yall.shape
    n_pages = page_table.shape[1]
    nh, hd = ATT_HEADS, HEAD_DIM
    assert (n_pages * PAGE_SIZE) % MOBA_BLOCK == 0 and tn <= MOBA_BLOCK and n_pages * PAGE_SIZE // MOBA_BLOCK >= MOBA_TOPK
    q = yall[:, :, COL_Q:COL_Q + ATT_WIDTH].reshape(b, tn, nh, hd).transpose(0, 2, 1, 3).reshape(b, nh * tn, hd)
    kn = yall[:, :, COL_K:COL_K + ATT_WIDTH].reshape(b, tn * nh, hd)
    vn = yall[:, :, COL_V:COL_V + ATT_WIDTH].reshape(b, tn * nh, hd)
    rows = PAGE_SIZE * nh
    grid_spec = pltpu.PrefetchScalarGridSpec(
        num_scalar_prefetch=1,
        grid=(b, n_pages),
        in_specs=[pl.BlockSpec((1, nh * tn, hd), lambda i, j, pt: (i, 0, 0)),
                  pl.BlockSpec((1, tn * nh, hd), lambda i, j, pt: (i, 0, 0)),
                  pl.BlockSpec((1, tn * nh, hd), lambda i, j, pt: (i, 0, 0)),
                  pl.BlockSpec((1, 1, rows, hd), lambda i, j, pt: (layer, pt[i, j], 0, 0)),
                  pl.BlockSpec((1, 1, rows, hd), lambda i, j, pt: (layer, pt[i, j], 0, 0))],
        out_specs=pl.BlockSpec((1, nh * tn, hd), lambda i, j, pt: (i, 0, 0)),
        scratch_shapes=[pltpu.VMEM((n_pages, nh * tn), f32), pltpu.VMEM((n_pages, nh * tn), f32),
                        pltpu.VMEM((n_pages, nh * tn, hd), f32), pltpu.VMEM((n_pages, nh, hd), f32)])
    out = pl.pallas_call(
        functools.partial(_moba_sample_kernel, n_pages=n_pages, n_new=tn),
        grid_spec=grid_spec,
        out_shape=jax.ShapeDtypeStruct((b, nh * tn, hd), f32),
        compiler_params=_params("arbitrary", "arbitrary"),
        name="moba_sample",
    )(page_table, q, kn, vn, cache_k, cache_v)
    return out.reshape(b, nh, tn, hd).transpose(0, 2, 1, 3).reshape(b, tn, ATT_WIDTH).astype(bf16)


def _merge_kernel(x_ref, g1_ref, ys5_ref, yatt_ref, yssd_ref, gs5_ref, gatt_ref, gssd_ref, ps5_ref, patt_ref,
                  pssd_ref, wo_ref, o_ref):
    bb, tt, _ = x_ref.shape
    m = bb * tt

    def branch(y_ref, g_ref, p_ref):
        y = y_ref[...].reshape(m, y_ref.shape[-1])
        return jax.nn.sigmoid(g_ref[...].reshape(m, D_MODEL)) * _dot(y, p_ref[...])

    merged = branch(ys5_ref, gs5_ref, ps5_ref) + branch(yatt_ref, gatt_ref, patt_ref) + branch(yssd_ref, gssd_ref, pssd_ref)
    out = _dot(merged.astype(bf16), wo_ref[...]).reshape(bb, tt, D_MODEL)
    o_ref[...] = x_ref[...] + g1_ref[...] * out


def _merge(x, g1, y_s5, y_att, y_ssd, yall, wl, bb, tt):
    b, t, _ = x.shape
    tok = lambda w: pl.BlockSpec((bb, tt, w), lambda i, j: (i, j, 0))
    gate = lambda col: pl.BlockSpec((bb, tt, D_MODEL), lambda i, j: (i, j, col // D_MODEL))
    wspec = lambda k: pl.BlockSpec((k, D_MODEL), lambda i, j: (0, 0))
    return pl.pallas_call(
        _merge_kernel,
        grid=(b // bb, t // tt),
        in_specs=[tok(D_MODEL), pl.BlockSpec((bb, 1, D_MODEL), lambda i, j: (i, 0, 0)),
                  tok(S5_WIDTH), tok(ATT_WIDTH), tok(SSD_WIDTH),
                  gate(COL_GS5), gate(COL_GATT), gate(COL_GSSD),
                  wspec(S5_WIDTH), wspec(ATT_WIDTH), wspec(SSD_WIDTH), wspec(D_MODEL)],
        out_specs=tok(D_MODEL),
        out_shape=jax.ShapeDtypeStruct(x.shape, f32),
        compiler_params=_params("arbitrary", "arbitrary"),
        name="merge",
    )(x, g1, y_s5, y_att, y_ssd, yall, yall, yall, wl['proj_s5'], wl['proj_att'], wl['proj_ssd'], wl['w_out'])


def _ffn_kernel(x_ref, nw_ref, sc_ref, sh_ref, g_ref, rw_ref, rb_ref, w1_ref, w3_ref, w2_ref, o_ref,
                h_ref, acc_ref, gate_ref, *, moe):
    bb, tt, _ = x_ref.shape
    m = bb * tt
    e = pl.program_id(2)
    f = pl.program_id(3)
    lane = lax.broadcasted_iota(jnp.int32, (m, 128), 1)

    @pl.when((e == 0) & (f == 0))
    def _():
        h = (_rms(x_ref[...], nw_ref[...]) * (1.0 + sc_ref[...]) + sh_ref[...]).reshape(m, D_MODEL)
        h_ref[...] = h.astype(bf16)
        acc_ref[...] = jnp.zeros(acc_ref.shape, f32)
        if moe:
            logits = jnp.dot(h, rw_ref[...], preferred_element_type=f32, precision=HIGHEST) + rb_ref[...]
            logits = jnp.where(lane < N_EXPERTS, logits, -jnp.inf)
            m1 = jnp.max(logits, axis=-1, keepdims=True)
            i1 = jnp.min(jnp.where(logits == m1, lane, 128), axis=-1, keepdims=True)
            rest = jnp.where(lane == i1, -jnp.inf, logits)
            m2 = jnp.max(rest, axis=-1, keepdims=True)
            i2 = jnp.min(jnp.where(rest == m2, lane, 128), axis=-1, keepdims=True)
            e2 = jnp.exp(m2 - m1)
            den = 1.0 + e2
            gate_ref[...] = jnp.where(lane == i1, 1.0 / den, 0.0) + jnp.where(lane == i2, e2 / den, 0.0)

    h = h_ref[...]
    act = (_silu(_dot(h, w1_ref[0])) * _dot(h, w3_ref[0])).astype(bf16)
    contrib = _dot(act, w2_ref[0])
    if moe:
        contrib = jnp.sum(jnp.where(lane == e, gate_ref[...], 0.0), axis=-1, keepdims=True) * contrib
    acc_ref[...] += contrib

    @pl.when((e == pl.num_programs(2) - 1) & (f == pl.num_programs(3) - 1))
    def _():
        o_ref[...] = x_ref[...] + g_ref[...] * acc_ref[...].reshape(bb, tt, D_MODEL)


def _ffn(x, nw, sc, sh, g, rw, rb, w1, w3, w2, bb, tt, moe):
    b, t, _ = x.shape
    ne = w1.shape[0]
    m = bb * tt
    tok = pl.BlockSpec((bb, tt, D_MODEL), lambda i, j, e, f: (i, j, 0))
    per_b = pl.BlockSpec((bb, 1, D_MODEL), lambda i, j, e, f: (i, 0, 0))
    return pl.pallas_call(
        functools.partial(_ffn_kernel, moe=moe),
        grid=(b // bb, t // tt, ne, D_FF // FFN_TF),
        in_specs=[tok, pl.BlockSpec((1, 1, D_MODEL), lambda i, j, e, f: (0, 0, 0)), per_b, per_b, per_b,
                  pl.BlockSpec((D_MODEL, 128), lambda i, j, e, f: (0, 0)),
                  pl.BlockSpec((1, 128), lambda i, j, e, f: (0, 0)),
                  pl.BlockSpec((1, D_MODEL, FFN_TF), lambda i, j, e, f: (e, 0, f)),
                  pl.BlockSpec((1, D_MODEL, FFN_TF), lambda i, j, e, f: (e, 0, f)),
                  pl.BlockSpec((1, FFN_TF, D_MODEL), lambda i, j, e, f: (e, f, 0))],
        out_specs=tok,
        out_shape=jax.ShapeDtypeStruct(x.shape, f32),
        scratch_shapes=[pltpu.VMEM((m, D_MODEL), bf16), pltpu.VMEM((m, D_MODEL), f32), pltpu.VMEM((m, 128), f32)],
        compiler_params=_params("arbitrary", "arbitrary", "arbitrary", "arbitrary"),
        name="ffn_moe" if moe else "ffn",
    )(x, nw.reshape(1, 1, D_MODEL), sc, sh, g, rw, rb, w1, w3, w2)


def _final_norm_kernel(x_ref, w_ref, o_ref):
    o_ref[...] = _rms(x_ref[...], w_ref[...])


def _final_norm(x, w, bb, tt):
    b, t, _ = x.shape
    tok = pl.BlockSpec((bb, tt, D_MODEL), lambda i, j: (i, j, 0))
    return pl.pallas_call(
        _final_norm_kernel,
        grid=(b // bb, t // tt),
        in_specs=[tok, pl.BlockSpec((1, 1, D_MODEL), lambda i, j: (0, 0, 0))],
        out_specs=tok,
        out_shape=jax.ShapeDtypeStruct(x.shape, f32),
        compiler_params=_params("arbitrary", "arbitrary"),
        name="final_norm",
    )(x, w.reshape(1, 1, D_MODEL))


def _layer(x, mod, wl, layer, tiles, s5_fn, moba_fn, ssd_fn):
    bb, tt = tiles
    sh1, sc1, g1, sh2, sc2, g2 = [mod[:, :, i * D_MODEL:(i + 1) * D_MODEL] for i in range(6)]
    yall = _norm_matmul(x, wl['norm1_w'], sc1, sh1, wl['w_main'], bb, tt, 768)
    dt_raw = _norm_matmul(x, wl['norm1_w'], sc1, sh1, wl['w_dt'], bb, tt, DT_PAD)[:, :, :SSD_HEADS]
    y_s5, s5_re, s5_im = s5_fn(yall[:, :, COL_U:COL_U + S5_WIDTH])
    y_att = moba_fn(yall)
    y_ssd, ssd_last = ssd_fn(yall, dt_raw)
    x = _merge(x, g1, y_s5, y_att, y_ssd, yall, wl, bb, tt // 2 if tt >= 16 else tt)
    fb, ft = (bb, tt // 2) if tt >= 16 else (bb // 2, tt)
    x = _ffn(x, wl['norm2_w'], sc2, sh2, g2, wl['router_w'], wl['router_b'], wl['w1'], wl['w3'], wl['w2'], fb, ft,
             wl['moe'])
    b, t, _ = x.shape
    k = yall[:, :, COL_K:COL_K + ATT_WIDTH].reshape(b, t, ATT_HEADS, HEAD_DIM)
    v = yall[:, :, COL_V:COL_V + ATT_WIDTH].reshape(b, t, ATT_HEADS, HEAD_DIM)
    conv_last = yall[:, t - (SSD_CONV - 1):, COL_XBC:COL_XBC + SSD_CONV_DIM]
    return x, (k, v, s5_re, s5_im, conv_last, ssd_last)


def kernel(x_prompt, x_sample, cache_k, cache_v, state_s5_re, state_s5_im, state_conv, state_ssd, page_table, c_prompt, c_sample, ada_w, ada_b, norm1_w, norm2_w, w_in, s5_a_re, s5_a_im, s5_log_dt, s5_b_re, s5_b_im, s5_c_re, s5_c_im, s5_d, s5_w_glu, s5_b_glu, ssd_conv_w, ssd_conv_b, ssd_dt_bias, ssd_a_log, ssd_d, ssd_norm_w, proj_s5, proj_att, proj_ssd, w_out, ffn_w1, ffn_w3, ffn_w2, router_w, router_b, moe_w1, moe_w3, moe_w2, final_norm_w):
    bp, tp, _ = x_prompt.shape
    bs, ts, _ = x_sample.shape
    assert tp % (8 * S5_SEG) == 0 and tp % MOBA_BLOCK == 0 and tp % SSD_CHUNK == 0 and ts >= SSD_CONV - 1

    n_c = bp + bs
    c_all = jnp.pad(jnp.concatenate([c_prompt, c_sample], axis=0), ((0, (-n_c) % 8), (0, 0)))
    mod_all = _ada(c_all, ada_w, ada_b)

    xbc0 = 3072
    dt0 = xbc0 + SSD_CONV_DIM
    g0 = dt0 + SSD_HEADS
    w_main = jnp.concatenate([w_in[:, :, :xbc0], w_in[:, :, g0:], w_in[:, :, xbc0:dt0]], axis=2).astype(bf16)
    w_dt = jnp.pad(w_in[:, :, dt0:g0], ((0, 0), (0, 0), (0, DT_PAD - SSD_HEADS))).astype(bf16)

    kc = cache_k.reshape(cache_k.shape[0], cache_k.shape[1], PAGE_SIZE * ATT_HEADS, HEAD_DIM)
    vc = cache_v.reshape(cache_v.shape[0], cache_v.shape[1], PAGE_SIZE * ATT_HEADS, HEAD_DIM)
    zero_pre = jnp.zeros((1, 1, SSD_CONV - 1, SSD_CONV_DIM), f32)
    zero_h0 = jnp.zeros((1, 1, SSD_HEADS, SSD_HEAD_DIM, SSD_STATE), f32)
    rw_pad = jnp.pad(router_w, ((0, 0), (0, 0), (0, 128 - N_EXPERTS)))
    rb_pad = jnp.pad(router_b, ((0, 0), (0, 128 - N_EXPERTS))).reshape(-1, 1, 128)

    xp, xs = x_prompt, x_sample
    st_p, st_s = [], []
    for l in range(DEPTH):
        pl_ = {'s5_a_re': s5_a_re[l], 's5_a_im': s5_a_im[l], 's5_log_dt': s5_log_dt[l], 's5_b_re': s5_b_re[l],
               's5_b_im': s5_b_im[l], 's5_c_re': s5_c_re[l], 's5_c_im': s5_c_im[l], 's5_d': s5_d[l],
               's5_w_glu': s5_w_glu[l], 's5_b_glu': s5_b_glu[l], 'ssd_conv_w': ssd_conv_w[l],
               'ssd_conv_b': ssd_conv_b[l], 'ssd_dt_bias': ssd_dt_bias[l], 'ssd_a_log': ssd_a_log[l],
               'ssd_d': ssd_d[l], 'ssd_norm_w': ssd_norm_w[l]}
        moe = l % 2 == 1
        wl = {'norm1_w': norm1_w[l], 'norm2_w': norm2_w[l], 'w_main': w_main[l], 'w_dt': w_dt[l],
              'proj_s5': proj_s5[l].astype(bf16), 'proj_att': proj_att[l].astype(bf16),
              'proj_ssd': proj_ssd[l].astype(bf16), 'w_out': w_out[l].astype(bf16), 'moe': moe,
              'router_w': rw_pad[l // 2] if moe else rw_pad[0], 'router_b': rb_pad[l // 2] if moe else rb_pad[0]}
        if moe:
            wl.update(w1=moe_w1[l // 2].astype(bf16), w3=moe_w3[l // 2].astype(bf16), w2=moe_w2[l // 2].astype(bf16))
        else:
            wl.update(w1=ffn_w1[l // 2][None].astype(bf16), w3=ffn_w3[l // 2][None].astype(bf16),
                      w2=ffn_w2[l // 2][None].astype(bf16))
        tb = _s5_tables(pl_)
        mod_p = mod_all[l, :bp].reshape(bp, 1, 6 * D_MODEL)
        mod_s = mod_all[l, bp:n_c].reshape(bs, 1, 6 * D_MODEL)

        xp, sp = _layer(
            xp, mod_p, wl, l, (1, 1024),
            lambda u: _s5_prompt(u, tb),
            _moba_prompt,
            lambda ya, dtr: _ssd_call(ya, dtr, zero_pre, zero_h0, l, pl_, SSD_CHUNK, True))
        xs, ss = _layer(
            xs, mod_s, wl, l, (bs, ts),
            lambda u: _s5_sample(u, state_s5_re[l], state_s5_im[l], tb),
            lambda ya: _moba_sample(ya, kc, vc, page_table, l),
            lambda ya, dtr: _ssd_call(ya, dtr, state_conv, state_ssd, l, pl_, math.gcd(ts, SSD_CHUNK), False))
        st_p.append(sp)
        st_s.append(ss)

    y_prompt = _final_norm(xp, final_norm_w, 1, 1024)
    y_sample = _final_norm(xs, final_norm_w, bs, ts)
    stack = lambda sts, i: jnp.stack([s[i] for s in sts])
    return (y_prompt, y_sample, stack(st_p, 0), stack(st_p, 1), stack(st_s, 0), stack(st_s, 1),
            stack(st_p, 2), stack(st_p, 3), stack(st_s, 2), stack(st_s, 3),
            stack(st_p, 4), stack(st_s, 4), stack(st_p, 5), stack(st_s, 5))
```
